```python
import jax, jax.numpy as jnp
from jax import lax
import numpy as np

D_MODEL = 1024
BATCH = 8
SEQ = 2048
DEPTH = 4
DEC_BATCH = 128
DEC_SEQ = 1
PAST_LEN = 2048
PAGE_SIZE = 128

N_MIXERS = 2
N_HEADS = 16
HEAD_DIM = D_MODEL // N_HEADS
D_FF = 4 * D_MODEL
Q_BLOCK = 128
RMS_EPS = 1e-6
N_FOX = (DEPTH + 1) // 2
N_SB = DEPTH // 2
ATTN_SCALE = HEAD_DIM ** -0.5

kernel_name = "fox_stickbreaking_hybrid_step"


def rmsnorm(x, g):
    xf = x.astype(jnp.float32)
    inv = lax.rsqrt(jnp.mean(xf * xf, axis=-1, keepdims=True) + RMS_EPS)
    return (xf * inv * g.astype(jnp.float32)).astype(x.dtype)


def fox_block(q, k, v, c):
    bq, L = q.shape[1], k.shape[1]
    s = jnp.einsum('bqhd,bkhd->bhqk', q, k, preferred_element_type=jnp.float32) * ATTN_SCALE
    ct = jnp.swapaxes(c.astype(jnp.float32), 1, 2)
    bias = ct[:, :, L - bq:, None] - ct[:, :, None, :]
    qpos = jnp.arange(L - bq, L)
    kpos = jnp.arange(L)
    mask = kpos[None, :] <= qpos[:, None]
    p = jax.nn.softmax(jnp.where(mask, s + bias, -jnp.inf), axis=-1)
    return jnp.einsum('bhqk,bkhd->bqhd', p.astype(v.dtype), v)


def sb_block(q, k, v):
    bq, L = q.shape[1], k.shape[1]
    z = jnp.einsum('bqhd,bkhd->bhqk', q, k, preferred_element_type=jnp.float32) * ATTN_SCALE
    qpos = jnp.arange(L - bq, L)
    kpos = jnp.arange(L)
    mask = kpos[None, :] < qpos[:, None]
    log_keep = jnp.where(mask, jax.nn.log_sigmoid(-z), 0.0)
    later = lax.cumsum(log_keep, axis=3, reverse=True) - log_keep
    a = jnp.where(mask, jnp.exp(jax.nn.log_sigmoid(z) + later), 0.0)
    return jnp.einsum('bhqk,bkhd->bqhd', a.astype(v.dtype), v)


def blocked_attention(block_fn, q, k, v, *side):
    tq = q.shape[1]
    q_start = k.shape[1] - tq
    outs = []
    for s0 in range(0, tq, Q_BLOCK):
        s1 = min(s0 + Q_BLOCK, tq)
        kend = q_start + s1
        outs.append(block_fn(q[:, s0:s1], k[:, :kend], v[:, :kend], *[a[:, :kend] for a in side]))
    return jnp.concatenate(outs, axis=1)


def split_heads(t, b, n):
    return t.reshape(b, n, N_HEADS, HEAD_DIM)


def fox_mixer(h, w_in, b_f, w_out, past_k=None, past_v=None, past_logf=None):
    b, n, _ = h.shape
    proj = h @ w_in
    q = split_heads(proj[..., :D_MODEL], b, n)
    k = split_heads(proj[..., D_MODEL:2 * D_MODEL], b, n)
    v = split_heads(proj[..., 2 * D_MODEL:3 * D_MODEL], b, n)
    logf = jax.nn.log_sigmoid(proj[..., 3 * D_MODEL:].astype(jnp.float32) + b_f.astype(jnp.float32))
    if past_k is None:
        k_all, v_all, logf_all = k, v, logf
    else:
        k_all = jnp.concatenate([past_k, k], axis=1)
        v_all = jnp.concatenate([past_v, v], axis=1)
        logf_all = jnp.concatenate([past_logf.astype(jnp.float32), logf], axis=1)
    c = jnp.cumsum(logf_all, axis=1)
    o = blocked_attention(fox_block, q, k_all, v_all, c)
    y = o.reshape(b, n, D_MODEL) @ w_out
    return y, k, v, logf.astype(h.dtype)


def sb_mixer(h, w_in, w_out, past_k=None, past_v=None):
    b, n, _ = h.shape
    proj = h @ w_in
    q = split_heads(proj[..., :D_MODEL], b, n)
    k = split_heads(proj[..., D_MODEL:2 * D_MODEL], b, n)
    v = split_heads(proj[..., 2 * D_MODEL:], b, n)
    if past_k is None:
        k_all, v_all = k, v
    else:
        k_all = jnp.concatenate([past_k, k], axis=1)
        v_all = jnp.concatenate([past_v, v], axis=1)
    o = blocked_attention(sb_block, q, k_all, v_all)
    y = o.reshape(b, n, D_MODEL) @ w_out
    return y, k, v


def sq_relu_mlp(h, w_up, w_down):
    return jnp.square(jax.nn.relu(h @ w_up)) @ w_down


def gather_pages(cache, layer, page_table):
    g = cache[layer, page_table]
    return g.reshape((page_table.shape[0], -1) + g.shape[3:])


def setup_inputs(seed: int = 0) -> dict:
    key = jax.random.key(seed)
    ks = jax.random.split(key, 20)
    n_pages = PAST_LEN // PAGE_SIZE
    n_used = DEC_BATCH * n_pages
    n_pool = n_used + max(1, n_used // 4)
    f32 = jnp.float32
    page_table = jax.random.permutation(ks[0], n_pool)[:n_used].reshape(DEC_BATCH, n_pages).astype(jnp.int32)
    kv_fox_shape = (N_FOX, n_pool, PAGE_SIZE, N_HEADS, HEAD_DIM)
    kv_sb_shape = (N_SB, n_pool, PAGE_SIZE, N_HEADS, HEAD_DIM)
    sd = D_MODEL ** -0.5
    return {
        "x_prompt": jax.random.normal(ks[1], (BATCH, SEQ, D_MODEL), f32),
        "x_sample": jax.random.normal(ks[2], (DEC_BATCH, DEC_SEQ, D_MODEL), f32),
        "cache_fox_k": jax.random.normal(ks[3], kv_fox_shape, f32),
        "cache_fox_v": jax.random.normal(ks[4], kv_fox_shape, f32),
        "cache_fox_logf": jax.nn.log_sigmoid(2.5 + 0.5 * jax.random.normal(ks[5], (N_FOX, n_pool, PAGE_SIZE, N_HEADS), f32)),
        "cache_sb_k": jax.random.normal(ks[6], kv_sb_shape, f32),
        "cache_sb_v": jax.random.normal(ks[7], kv_sb_shape, f32),
        "page_table": page_table,
        "norm_mix": 1.0 + 0.02 * jax.random.normal(ks[8], (DEPTH, D_MODEL), f32),
        "norm_mlp": 1.0 + 0.02 * jax.random.normal(ks[9], (DEPTH, D_MODEL), f32),
        "norm_final": 1.0 + 0.02 * jax.random.normal(ks[10], (D_MODEL,), f32),
        "fox_w_in": sd * jax.random.normal(ks[11], (N_FOX, D_MODEL, 3 * D_MODEL + N_HEADS), f32),
        "fox_b_f": jax.random.uniform(ks[12], (N_FOX, N_HEADS), f32, minval=1.0, maxval=4.0),
        "fox_w_out": sd * jax.random.normal(ks[13], (N_FOX, D_MODEL, D_MODEL), f32),
        "sb_w_in": sd * jax.random.normal(ks[14], (N_SB, D_MODEL, 3 * D_MODEL), f32),
        "sb_w_out": sd * jax.random.normal(ks[15], (N_SB, D_MODEL, D_MODEL), f32),
        "mlp_w_up": sd * jax.random.normal(ks[16], (DEPTH, D_MODEL, D_FF), f32),
        "mlp_w_down": (D_FF ** -0.5) * jax.random.normal(ks[17], (DEPTH, D_FF, D_MODEL), f32),
    }


def reference(x_prompt, x_sample, cache_fox_k, cache_fox_v, cache_fox_logf, cache_sb_k, cache_sb_v,
              page_table, norm_mix, norm_mlp, norm_final, fox_w_in, fox_b_f, fox_w_out,
              sb_w_in, sb_w_out, mlp_w_up, mlp_w_down):
    xp, xs = x_prompt, x_sample
    fkp, fvp, flp, fks, fvs, fls = [], [], [], [], [], []
    skp, svp, sks, svs = [], [], [], []
    for l in range(DEPTH):
        hp = rmsnorm(xp, norm_mix[l])
        hs = rmsnorm(xs, norm_mix[l])
        if l % N_MIXERS == 0:
            f = l // N_MIXERS
            yp, kp, vp, lp = fox_mixer(hp, fox_w_in[f], fox_b_f[f], fox_w_out[f])
            ys, kn, vn, ln = fox_mixer(hs, fox_w_in[f], fox_b_f[f], fox_w_out[f],
                                       gather_pages(cache_fox_k, f, page_table),
                                       gather_pages(cache_fox_v, f, page_table),
                                       gather_pages(cache_fox_logf, f, page_table))
            fkp.append(kp); fvp.append(vp); flp.append(lp)
            fks.append(kn); fvs.append(vn); fls.append(ln)
        else:
            m = l // N_MIXERS
            yp, kp, vp = sb_mixer(hp, sb_w_in[m], sb_w_out[m])
            ys, kn, vn = sb_mixer(hs, sb_w_in[m], sb_w_out[m],
                                  gather_pages(cache_sb_k, m, page_table),
                                  gather_pages(cache_sb_v, m, page_table))
            skp.append(kp); svp.append(vp)
            sks.append(kn); svs.append(vn)
        xp = xp + yp
        xs = xs + ys
        xp = xp + sq_relu_mlp(rmsnorm(xp, norm_mlp[l]), mlp_w_up[l], mlp_w_down[l])
        xs = xs + sq_relu_mlp(rmsnorm(xs, norm_mlp[l]), mlp_w_up[l], mlp_w_down[l])
    y_prompt = rmsnorm(xp, norm_final)
    y_sample = rmsnorm(xs, norm_final)
    return (y_prompt, y_sample,
            jnp.stack(fkp), jnp.stack(fvp), jnp.stack(flp), jnp.stack(skp), jnp.stack(svp),
            jnp.stack(fks), jnp.stack(fvs), jnp.stack(fls), jnp.stack(sks), jnp.stack(svs))
```

```python
import functools

import jax
import jax.numpy as jnp
from jax import lax
from jax.experimental import pallas as pl
from jax.experimental.pallas import tpu as pltpu

F32 = jnp.float32
BF16 = jnp.bfloat16

N_HEADS = 16
HEAD_DIM = 64
RMS_EPS = 1e-6
ATTN_SCALE = HEAD_DIM ** -0.5
LANES = 128
NEG_BIG = -1e30
VMEM_LIMIT = 56 * 1024 * 1024
NT_DIMS = (((1,), (1,)), ((), ()))


def _rms(x, g):
    inv = lax.rsqrt(jnp.mean(x * x, axis=-1, keepdims=True) + RMS_EPS)
    return x * inv * g


def _log_sigmoid(x):
    return jnp.minimum(x, 0.0) - jnp.log1p(jnp.exp(-jnp.abs(x)))


def _split_bf16(x):
    hi = x.astype(BF16)
    return hi, (x - hi.astype(F32)).astype(BF16)


def _pick_tile(n, candidates):
    for c in candidates:
        if n % c == 0:
            return c
    return n


def _resident(shape):
    return pl.BlockSpec(shape, lambda *_: (0,) * len(shape), pipeline_mode=pl.Buffered(1))


def _params(*sem):
    return pltpu.CompilerParams(dimension_semantics=sem, vmem_limit_bytes=VMEM_LIMIT)


def _inproj_body(*refs, fox, d):
    if fox:
        (x_ref, g_ref, w_ref, wf_ref, wft_ref, bf_ref, bft_ref,
         q_ref, k_ref, v_ref, lf_ref, lft_ref) = refs
    else:
        x_ref, g_ref, w_ref, q_ref, k_ref, v_ref = refs
    h = _rms(x_ref[...], g_ref[...]).astype(BF16)
    q = jnp.dot(h, w_ref[:, 0:d], preferred_element_type=F32)
    q_ref[...] = (q * ATTN_SCALE).astype(q_ref.dtype)
    k_ref[...] = jnp.dot(h, w_ref[:, d:2 * d], preferred_element_type=F32)
    v_ref[...] = jnp.dot(h, w_ref[:, 2 * d:3 * d], preferred_element_type=F32)
    if fox:
        f = jnp.dot(h, wf_ref[...], preferred_element_type=F32) + bf_ref[...]
        lf_ref[...] = _log_sigmoid(f)
        ft = lax.dot_general(wft_ref[...], h, NT_DIMS, preferred_element_type=F32)
        lft_ref[...] = _log_sigmoid(ft + bft_ref[...])


def _inproj(x, g, w_qkv, fox_extra, q_dtype):
    t, d = x.shape
    tm = _pick_tile(t, (512, 256, 128, 64, 32, 16, 8))
    fox = fox_extra is not None
    row = lambda i: (i, 0)
    fixed = lambda i: (0, 0)
    in_specs = [pl.BlockSpec((tm, d), row), pl.BlockSpec((1, d), fixed), _resident((d, 3 * d))]
    out_shape = [jax.ShapeDtypeStruct((t, d), q_dtype), jax.ShapeDtypeStruct((t, d), F32),
                 jax.ShapeDtypeStruct((t, d), F32)]
    out_specs = [pl.BlockSpec((tm, d), row)] * 3
    args = [x, g.reshape(1, d), w_qkv]
    if fox:
        wf, wft, bf, bft = fox_extra
        nh = wf.shape[1]
        in_specs += [pl.BlockSpec((d, nh), fixed), pl.BlockSpec((nh, d), fixed),
                     pl.BlockSpec((1, nh), fixed), pl.BlockSpec((nh, 1), fixed)]
        args += [wf, wft, bf, bft]
        out_shape += [jax.ShapeDtypeStruct((t, nh), F32), jax.ShapeDtypeStruct((nh, t), F32)]
        out_specs += [pl.BlockSpec((tm, nh), row), pl.BlockSpec((nh, tm), lambda i: (0, i))]
    return pl.pallas_call(
        functools.partial(_inproj_body, fox=fox, d=d),
        grid=(t // tm,), in_specs=in_specs, out_specs=out_specs, out_shape=out_shape,
        compiler_params=_params("parallel"), name="inproj_fox" if fox else "inproj_sb",
    )(*args)


CS_BLOCK = 256


def _cumsum_body(lt_ref, ct_ref, *, s):
    r = lax.broadcasted_iota(jnp.int32, (CS_BLOCK, CS_BLOCK), 0)
    c = lax.broadcasted_iota(jnp.int32, (CS_BLOCK, CS_BLOCK), 1)
    upper = jnp.where(r <= c, 1.0, 0.0).astype(F32)
    carry = jnp.zeros((lt_ref.shape[0], 1), F32)
    for j in range(s // CS_BLOCK):
        blk = lt_ref[:, j * CS_BLOCK:(j + 1) * CS_BLOCK]
        cs = jnp.dot(blk, upper, precision=lax.Precision.HIGHEST, preferred_element_type=F32) + carry
        ct_ref[:, j * CS_BLOCK:(j + 1) * CS_BLOCK] = cs
        carry = cs[:, CS_BLOCK - 1:CS_BLOCK]


def _seq_cumsum(lft, b, s):
    nh = lft.shape[0]
    spec = pl.BlockSpec((nh, s), lambda i: (0, i))
    return pl.pallas_call(
        functools.partial(_cumsum_body, s=s), grid=(b,), in_specs=[spec], out_specs=spec,
        out_shape=jax.ShapeDtypeStruct(lft.shape, F32), compiler_params=_params("parallel"), name="fox_cumsum",
    )(lft)


def _stack_heads(q_ref, tq):
    lane = lax.broadcasted_iota(jnp.int32, (tq, LANES), 1)
    q = q_ref[...].astype(F32)
    qa = jnp.where(lane < HEAD_DIM, q, 0.0).astype(BF16)
    qb = jnp.where(lane >= HEAD_DIM, q, 0.0).astype(BF16)
    return jnp.concatenate([qa, qb], axis=0)


def _unstack_heads(acc, tq):
    lane = lax.broadcasted_iota(jnp.int32, (tq, LANES), 1)
    return jnp.where(lane < HEAD_DIM, acc[0:tq], acc[tq:2 * tq])


def _fox_attn_body(q_ref, k_ref, v_ref, ct_ref, o_ref, m_ref, l_ref, acc_ref, *, tq):
    qi = pl.program_id(2)
    tk = tq
    reps = tk // LANES
    qs = _stack_heads(q_ref, tq)
    q0 = pl.multiple_of(qi * tq, tq)
    cend = ct_ref[:, pl.ds(q0, tq)][:, tq - 1:tq]
    m_ref[...] = jnp.full(m_ref.shape, NEG_BIG, F32)
    l_ref[...] = jnp.zeros(l_ref.shape, F32)
    acc_ref[...] = jnp.zeros(acc_ref.shape, F32)
    row = lax.broadcasted_iota(jnp.int32, (tq, tk), 0)
    col = lax.broadcasted_iota(jnp.int32, (tq, tk), 1)

    def step(j0, masked):
        kb = k_ref[pl.ds(j0, tk), :].astype(BF16)
        vb = v_ref[pl.ds(j0, tk), :].astype(BF16)
        s = lax.dot_general(qs, kb, NT_DIMS, preferred_element_type=F32)
        bias = cend - ct_ref[:, pl.ds(j0, tk)]
        ps, alphas = [], []
        for hh in range(2):
            rows = slice(hh * tq, (hh + 1) * tq)
            sh = s[rows] + bias[hh:hh + 1, :]
            if masked:
                sh = jnp.where(col <= row, sh, NEG_BIG)
            m_prev = m_ref[rows]
            m_new = jnp.maximum(m_prev, jnp.max(sh, axis=1, keepdims=True))
            alpha = jnp.exp(m_prev - m_new)
            p = jnp.exp(sh - jnp.tile(m_new, (1, reps)))
            l_ref[rows] = alpha * l_ref[rows] + jnp.sum(p, axis=1, keepdims=True)
            m_ref[rows] = m_new
            ps.append(p.astype(BF16))
            alphas.append(alpha)
        pv = jnp.dot(jnp.concatenate(ps, axis=0), vb, preferred_element_type=F32)
        acc_ref[...] = jnp.concatenate(alphas, axis=0) * acc_ref[...] + pv

    def full_block(j, carry):
        step(pl.multiple_of(j * tk, tk), False)
        return carry

    lax.fori_loop(0, qi, full_block, 0)
    step(q0, True)
    o_ref[...] = _unstack_heads(acc_ref[...] / l_ref[...], tq).astype(o_ref.dtype)


def _sb_attn_body(q_ref, k_ref, v_ref, o_ref, r_ref, acc_ref, *, tq):
    qi = pl.program_id(2)
    tk = tq
    reps = tk // LANES
    qs = _stack_heads(q_ref, tq)
    q0 = pl.multiple_of(qi * tq, tq)
    r_ref[...] = jnp.zeros(r_ref.shape, F32)
    acc_ref[...] = jnp.zeros(acc_ref.shape, F32)
    row = lax.broadcasted_iota(jnp.int32, (tq, tk), 0)
    col = lax.broadcasted_iota(jnp.int32, (tq, tk), 1)
    strict_lower = jnp.where(row > col, 1.0, 0.0).astype(BF16)
    row2 = lax.broadcasted_iota(jnp.int32, (2 * tq, tk), 0)
    col2 = lax.broadcasted_iota(jnp.int32, (2 * tq, tk), 1)
    valid = col2 < jnp.where(row2 >= tq, row2 - tq, row2)

    def step(j0, masked):
        kb = k_ref[pl.ds(j0, tk), :].astype(BF16)
        vb = v_ref[pl.ds(j0, tk), :].astype(BF16)
        z = lax.dot_general(qs, kb, NT_DIMS, preferred_element_type=F32)
        lk = _log_sigmoid(-z)
        if masked:
            lk = jnp.where(valid, lk, 0.0)
        hi, lo = _split_bf16(lk)
        later = (jnp.dot(hi, strict_lower, preferred_element_type=F32)
                 + jnp.dot(lo, strict_lower, preferred_element_type=F32))
        r = r_ref[...]
        a = jnp.exp(lk + z + later + jnp.tile(r, (1, reps)))
        if masked:
            a = jnp.where(valid, a, 0.0)
        acc_ref[...] += jnp.dot(a.astype(BF16), vb, preferred_element_type=F32)
        r_ref[...] = r + jnp.sum(lk, axis=1, keepdims=True)

    step(q0, True)

    def full_block(i, carry):
        step(pl.multiple_of((qi - 1 - i) * tk, tk), False)
        return carry

    lax.fori_loop(0, qi, full_block, 0)
    o_ref[...] = _unstack_heads(acc_ref[...], tq).astype(o_ref.dtype)


def _prompt_attn(q, k, v, ct, b, s):
    t, d = q.shape
    hp = d // LANES
    tq = _pick_tile(s, (256, 128))
    nq = s // tq
    q3, k3, v3 = (a.reshape(b, s, d) for a in (q, k, v))
    q_spec = pl.BlockSpec((None, tq, LANES), lambda bi, hi, qi: (bi, qi, hi))
    kv_spec = pl.BlockSpec((None, s, LANES), lambda bi, hi, qi: (bi, 0, hi))
    in_specs = [q_spec, kv_spec, kv_spec]
    args = [q3, k3, v3]
    stat = pltpu.VMEM((2 * tq, LANES), F32)
    if ct is not None:
        in_specs.append(pl.BlockSpec((None, 2, s), lambda bi, hi, qi: (hi, 0, bi)))
        args.append(ct.reshape(hp, 2, t))
        body, scratch, name = _fox_attn_body, [stat, stat, stat], "fox_attn"
    else:
        body, scratch, name = _sb_attn_body, [stat, stat], "sb_attn"
    o = pl.pallas_call(
        functools.partial(body, tq=tq), grid=(b, hp, nq), in_specs=in_specs, out_specs=q_spec,
        out_shape=jax.ShapeDtypeStruct((b, s, d), BF16), scratch_shapes=scratch,
        compiler_params=_params("parallel", "parallel", "arbitrary"), name=name,
    )(*args)
    return o.reshape(t, d)


def _mlp_body(o_ref, x_ref, wo_ref, g_ref, wu_ref, wd_ref, gf_ref, y_ref, *, final, chunk):
    x1 = x_ref[...] + jnp.dot(o_ref[...].astype(BF16), wo_ref[...], preferred_element_type=F32)
    h = _rms(x1, g_ref[...]).astype(BF16)
    acc = x1
    for c in range(wu_ref.shape[1] // chunk):
        up = jnp.dot(h, wu_ref[:, c * chunk:(c + 1) * chunk], preferred_element_type=F32)
        act = jnp.square(jnp.maximum(up, 0.0)).astype(BF16)
        acc = acc + jnp.dot(act, wd_ref[c * chunk:(c + 1) * chunk, :], preferred_element_type=F32)
    if final:
        acc = _rms(acc, gf_ref[...])
    y_ref[...] = acc


def _outproj_mlp(o, x, w_out, g, w_up, w_down, g_final, final):
    t, d = x.shape
    dff = w_up.shape[1]
    tm = _pick_tile(t, (512, 256, 128, 64, 32, 16, 8))
    row = lambda i: (i, 0)
    fixed = lambda i: (0, 0)
    return pl.pallas_call(
        functools.partial(_mlp_body, final=final, chunk=min(dff, 1024)),
        grid=(t // tm,),
        in_specs=[pl.BlockSpec((tm, d), row), pl.BlockSpec((tm, d), row), _resident((d, d)),
                  pl.BlockSpec((1, d), fixed), _resident((d, dff)), _resident((dff, d)),
                  pl.BlockSpec((1, d), fixed)],
        out_specs=pl.BlockSpec((tm, d), row), out_shape=jax.ShapeDtypeStruct((t, d), F32),
        compiler_params=_params("parallel"), name="outproj_mlp",
    )(o, x, w_out, g.reshape(1, d), w_up, w_down, g_final.reshape(1, d))


def _head_mask(d):
    head = lax.broadcasted_iota(jnp.int32, (N_HEADS, d), 0)
    lane = lax.broadcasted_iota(jnp.int32, (N_HEADS, d), 1)
    return lax.shift_right_logical(lane, 6) == head


def _page_copies(pt_ref, b, slot, layer, n_pages, pairs, sems):
    cps = []
    for j in range(n_pages):
        pg = pt_ref[b * n_pages + j]
        for n, (src, dst) in enumerate(pairs):
            cps.append(pltpu.make_async_copy(src.at[layer, pg], dst.at[slot, j], sems.at[n, slot]))
    return cps


def _prefetch_pages(copies_for, b, nb):
    slot = b % 2

    @pl.when(b == 0)
    def _():
        for c in copies_for(0, 0):
            c.start()

    @pl.when(b + 1 < nb)
    def _():
        for c in copies_for(b + 1, 1 - slot):
            c.start()

    for c in copies_for(b, slot):
        c.wait()
    return slot


def _dec_fox_body(pt_ref, q_ref, kn_ref, vn_ref, lfn_ref, ck_hbm, cv_hbm, lf_hbm, o_ref,
                  kbuf, vbuf, lfbuf, sems, *, layer, n_pages, page):
    b = pl.program_id(0)
    d = q_ref.shape[-1]
    pairs = [(ck_hbm, kbuf), (cv_hbm, vbuf), (lf_hbm, lfbuf)]
    slot = _prefetch_pages(lambda bb, sl: _page_copies(pt_ref, bb, sl, layer, n_pages, pairs, sems),
                           b, pl.num_programs(0))
    hmask = _head_mask(d)
    qm = jnp.where(hmask, jnp.broadcast_to(q_ref[...], (N_HEADS, d)), 0.0)
    r = lax.broadcasted_iota(jnp.int32, (page, page), 0)
    c = lax.broadcasted_iota(jnp.int32, (page, page), 1)
    lower_incl = jnp.where(r >= c, 1.0, 0.0).astype(BF16)
    lf_all = jnp.concatenate([lfbuf[slot, j] for j in range(n_pages)], axis=0)
    hi, lo = _split_bf16(lf_all)
    incl = (jnp.dot(hi, lower_incl, preferred_element_type=F32)
            + jnp.dot(lo, lower_incl, preferred_element_type=F32))
    excl = incl - lf_all
    suffix = jnp.zeros((N_HEADS, 1), F32)
    logits = [None] * n_pages
    for j in reversed(range(n_pages)):
        rows = slice(j * N_HEADS, (j + 1) * N_HEADS)
        s_j = jnp.dot(qm, kbuf[slot, j], preferred_element_type=F32)
        logits[j] = s_j + excl[rows] + suffix
        suffix = suffix + incl[rows][:, 0:1]
    kn = jnp.broadcast_to(kn_ref[...], (8, d))
    s_new = lax.dot_general(qm, kn, NT_DIMS, preferred_element_type=F32)[:, 0:1] - lfn_ref[...]
    m = s_new
    for lg in logits:
        m = jnp.maximum(m, jnp.max(lg, axis=1, keepdims=True))
    p_new = jnp.exp(s_new - m)
    denom = p_new
    out = p_new * vn_ref[...]
    for j, lg in enumerate(logits):
        p = jnp.exp(lg - m)
        denom = denom + jnp.sum(p, axis=1, keepdims=True)
        out = out + lax.dot_general(p, vbuf[slot, j], NT_DIMS, preferred_element_type=F32)
    out = jnp.where(hmask, out / denom, 0.0)
    o_ref[...] = jnp.sum(out, axis=0, keepdims=True)


def _dec_sb_body(pt_ref, q_ref, ck_hbm, cv_hbm, o_ref, kbuf, vbuf, sems, *, layer, n_pages, page):
    b = pl.program_id(0)
    d = q_ref.shape[-1]
    pairs = [(ck_hbm, kbuf), (cv_hbm, vbuf)]
    slot = _prefetch_pages(lambda bb, sl: _page_copies(pt_ref, bb, sl, layer, n_pages, pairs, sems),
                           b, pl.num_programs(0))
    hmask = _head_mask(d)
    qm = jnp.where(hmask, jnp.broadcast_to(q_ref[...], (N_HEADS, d)), 0.0)
    z_all = jnp.concatenate([jnp.dot(qm, kbuf[slot, j], preferred_element_type=F32)
                             for j in range(n_pages)], axis=0)
    lk_all = _log_sigmoid(-z_all)
    r = lax.broadcasted_iota(jnp.int32, (page, page), 0)
    c = lax.broadcasted_iota(jnp.int32, (page, page), 1)
    strict_lower = jnp.where(r > c, 1.0, 0.0).astype(BF16)
    hi, lo = _split_bf16(lk_all)
    later_all = (jnp.dot(hi, strict_lower, preferred_element_type=F32)
                 + jnp.dot(lo, strict_lower, preferred_element_type=F32))
    e_all = lk_all + z_all + later_all
    page_tot = jnp.sum(lk_all, axis=1, keepdims=True)
    later_pages = jnp.zeros((N_HEADS, 1), F32)
    out = jnp.zeros((N_HEADS, d), F32)
    for j in reversed(range(n_pages)):
        rows = slice(j * N_HEADS, (j + 1) * N_HEADS)
        a = jnp.exp(e_all[rows] + later_pages)
        out = out + lax.dot_general(a, vbuf[slot, j], NT_DIMS, preferred_element_type=F32)
        later_pages = later_pages + page_tot[rows]
    o_ref[...] = jnp.sum(jnp.where(hmask, out, 0.0), axis=0, keepdims=True)


def _decode_attn(page_table, q, cache_k, cache_v, layer, fox_extra):
    db, d = q.shape
    n_pages = page_table.shape[1]
    page = cache_k.shape[-1]
    vec = pl.BlockSpec((None, 1, d), lambda b, pt: (b, 0, 0))
    hbm = pl.BlockSpec(memory_space=pl.ANY)
    kv_buf = pltpu.VMEM((2, n_pages, d, page), F32)
    fox = fox_extra is not None
    if fox:
        k_new, v_new, lf_new, cache_lf = fox_extra
        nh = lf_new.shape[1]
        in_specs = [vec, vec, vec, pl.BlockSpec((None, nh, 1), lambda b, pt: (b, 0, 0)), hbm, hbm, hbm]
        args = [q.reshape(db, 1, d), k_new.reshape(db, 1, d), v_new.reshape(db, 1, d),
                lf_new.reshape(db, nh, 1), cache_k, cache_v, cache_lf]
        scratch = [kv_buf, kv_buf, pltpu.VMEM((2, n_pages, nh, page), F32), pltpu.SemaphoreType.DMA((3, 2))]
        body, name = _dec_fox_body, "decode_fox"
    else:
        in_specs = [vec, hbm, hbm]
        args = [q.reshape(db, 1, d), cache_k, cache_v]
        scratch = [kv_buf, kv_buf, pltpu.SemaphoreType.DMA((2, 2))]
        body, name = _dec_sb_body, "decode_sb"
    o = pl.pallas_call(
        functools.partial(body, layer=layer, n_pages=n_pages, page=page),
        grid_spec=pltpu.PrefetchScalarGridSpec(
            num_scalar_prefetch=1, grid=(db,), in_specs=in_specs, out_specs=vec, scratch_shapes=scratch),
        out_shape=jax.ShapeDtypeStruct((db, 1, d), F32),
        compiler_params=_params("arbitrary"), name=name,
    )(page_table.reshape(-1), *args)
    return o.reshape(db, d)


def _feature_major(cache):
    n, pool, page, nh, dh = cache.shape
    return jnp.transpose(cache, (0, 1, 3, 4, 2)).reshape(n, pool, nh * dh, page)


def kernel(x_prompt, x_sample, cache_fox_k, cache_fox_v, cache_fox_logf, cache_sb_k, cache_sb_v, page_table,
           norm_mix, norm_mlp, norm_final, fox_w_in, fox_b_f, fox_w_out, sb_w_in, sb_w_out, mlp_w_up,
           mlp_w_down):
    b, s, d = x_prompt.shape
    db = x_sample.shape[0]
    depth = norm_mix.shape[0]
    nh = fox_b_f.shape[1]
    xp = x_prompt.reshape(b * s, d)
    xs = x_sample.reshape(db, d)
    fox_kt, fox_vt, sb_kt, sb_vt = (_feature_major(c) for c in (cache_fox_k, cache_fox_v, cache_sb_k, cache_sb_v))
    fox_lft = jnp.transpose(cache_fox_logf, (0, 1, 3, 2))
    outs = {name: [] for name in ("fkp", "fvp", "flp", "skp", "svp", "fks", "fvs", "fls", "sks", "svs")}
    for l in range(depth):
        i = l // 2
        final = l == depth - 1
        if l % 2 == 0:
            w_in = fox_w_in[i]
            w_qkv = w_in[:, :3 * d].astype(BF16)
            wf = w_in[:, 3 * d:].astype(BF16)
            extra = (wf, wf.T, fox_b_f[i].reshape(1, nh), fox_b_f[i].reshape(nh, 1))
            qp, kp, vp, lfp, lftp = _inproj(xp, norm_mix[l], w_qkv, extra, BF16)
            qs, ks, vs, lfs, _ = _inproj(xs, norm_mix[l], w_qkv, extra, F32)
            op = _prompt_attn(qp, kp, vp, _seq_cumsum(lftp, b, s), b, s)
            os_ = _decode_attn(page_table, qs, fox_kt, fox_vt, i, (ks, vs, lfs, fox_lft))
            w_out = fox_w_out[i]
            outs["fkp"].append(kp); outs["fvp"].append(vp); outs["flp"].append(lfp)
            outs["fks"].append(ks); outs["fvs"].append(vs); outs["fls"].append(lfs)
        else:
            w_qkv = sb_w_in[i].astype(BF16)
            qp, kp, vp = _inproj(xp, norm_mix[l], w_qkv, None, BF16)
            qs, ks, vs = _inproj(xs, norm_mix[l], w_qkv, None, F32)
            op = _prompt_attn(qp, kp, vp, None, b, s)
            os_ = _decode_attn(page_table, qs, sb_kt, sb_vt, i, None)
            w_out = sb_w_out[i]
            outs["skp"].append(kp); outs["svp"].append(vp)
            outs["sks"].append(ks); outs["svs"].append(vs)
        w_out, w_up, w_down = w_out.astype(BF16), mlp_w_up[l].astype(BF16), mlp_w_down[l].astype(BF16)
        xp = _outproj_mlp(op, xp, w_out, norm_mlp[l], w_up, w_down, norm_final, final)
        xs = _outproj_mlp(os_, xs, w_out, norm_mlp[l], w_up, w_down, norm_final, final)

    def kv(name, rows):
        return jnp.stack(outs[name]).reshape((-1,) + rows + (N_HEADS, HEAD_DIM))

    def lf(name, rows):
        return jnp.stack(outs[name]).reshape((-1,) + rows + (nh,))

    return (xp.reshape(b, s, d), xs.reshape(db, 1, d),
            kv("fkp", (b, s)), kv("fvp", (b, s)), lf("flp", (b, s)), kv("skp", (b, s)), kv("svp", (b, s)),
            kv("fks", (db, 1)), kv("fvs", (db, 1)), lf("fls", (db, 1)), kv("sks", (db, 1)), kv("svs", (db, 1)))
```

```python
import functools

import jax
import jax.numpy as jnp
from jax import lax
from jax.experimental import pallas as pl
from jax.experimental.pallas import tpu as pltpu

F32 = jnp.float32
BF16 = jnp.bfloat16

N_HEADS = 16
HEAD_DIM = 64
RMS_EPS = 1e-6
ATTN_SCALE = HEAD_DIM ** -0.5
LANES = 128
NEG_BIG = -1e30
SB_CUTOFF = -105.0
VMEM_LIMIT = 56 * 1024 * 1024
NT_DIMS = (((1,), (1,)), ((), ()))


def _rms(x, g):
    inv = lax.rsqrt(jnp.mean(x * x, axis=-1, keepdims=True) + RMS_EPS)
    return x * inv * g


def _log_sigmoid(x):
    return jnp.minimum(x, 0.0) - jnp.log(1.0 + jnp.exp(-jnp.abs(x)))


def _split_bf16(x):
    hi = x.astype(BF16)
    return hi, (x - hi.astype(F32)).astype(BF16)


def _nt(a, b):
    return lax.dot_general(a, b, NT_DIMS, preferred_element_type=F32)


def _pick_tile(n, candidates):
    for c in candidates:
        if n % c == 0:
            return c
    return n


def _resident(shape):
    return pl.BlockSpec(shape, lambda *_: (0,) * len(shape), pipeline_mode=pl.Buffered(1))


def _params(*sem):
    return pltpu.CompilerParams(dimension_semantics=sem, vmem_limit_bytes=VMEM_LIMIT)


def _inproj_prompt_body(*refs, fox, d, n_alias):
    n_in = 5 if fox else 3
    x_ref, g_ref, wt_ref = refs[:3]
    outs = refs[n_in + n_alias:]
    h = _rms(x_ref[...], g_ref[...]).astype(BF16)
    outs[0][...] = (_nt(h, wt_ref[0:d, :]) * ATTN_SCALE).astype(outs[0].dtype)
    outs[1][...] = _nt(wt_ref[d:2 * d, :], h)
    outs[2][...] = _nt(wt_ref[2 * d:3 * d, :], h)
    if fox:
        wft_ref, bft_ref = refs[3:5]
        outs[3][...] = _log_sigmoid(_nt(wft_ref[...], h) + bft_ref[...])


def _inproj_prompt(x, g, wt, fox_extra, stacks, layer, b, s):
    t, d = x.shape
    tm = _pick_tile(s, (512, 256, 128))
    nt = s // tm
    fox = fox_extra is not None
    fixed = lambda i: (0, 0)
    slab = lambda i: (layer, i // nt, 0, i % nt)
    in_specs = [pl.BlockSpec((tm, d), lambda i: (i, 0)), pl.BlockSpec((1, d), fixed), _resident(wt.shape)]
    args = [x, g.reshape(1, d), wt]
    if fox:
        wft, bft = fox_extra
        in_specs += [pl.BlockSpec(wft.shape, fixed), pl.BlockSpec(bft.shape, fixed)]
        args += [wft, bft]
    n_in = len(args)
    in_specs += [pl.BlockSpec(memory_space=pl.ANY)] * len(stacks)
    out_shape = [jax.ShapeDtypeStruct((t, d), BF16)] + [jax.ShapeDtypeStruct(a.shape, a.dtype) for a in stacks]
    out_specs = [pl.BlockSpec((tm, d), lambda i: (i, 0))]
    out_specs += [pl.BlockSpec((None, None, a.shape[2], tm), slab) for a in stacks]
    return pl.pallas_call(
        functools.partial(_inproj_prompt_body, fox=fox, d=d, n_alias=len(stacks)),
        grid=(t // tm,), in_specs=in_specs, out_specs=out_specs, out_shape=out_shape,
        input_output_aliases={n_in + n: 1 + n for n in range(len(stacks))},
        compiler_params=_params("parallel"), name="inproj_prompt_fox" if fox else "inproj_prompt_sb",
    )(*args, *stacks)


def _inproj_sample_body(*refs, fox, d):
    if fox:
        x_ref, g_ref, wt_ref, wft_ref, bf_ref, q_ref, k_ref, v_ref, lf_ref = refs
    else:
        x_ref, g_ref, wt_ref, q_ref, k_ref, v_ref = refs
    h = _rms(x_ref[...], g_ref[...]).astype(BF16)
    q_ref[...] = _nt(h, wt_ref[0:d, :]) * ATTN_SCALE
    k_ref[...] = _nt(h, wt_ref[d:2 * d, :])
    v_ref[...] = _nt(h, wt_ref[2 * d:3 * d, :])
    if fox:
        lf_ref[...] = _log_sigmoid(_nt(h, wft_ref[...]) + bf_ref[...])


def _inproj_sample(x, g, wt, fox_extra):
    t, d = x.shape
    tm = _pick_tile(t, (512, 256, 128, 64, 32, 16, 8))
    fox = fox_extra is not None
    row = lambda i: (i, 0)
    fixed = lambda i: (0, 0)
    in_specs = [pl.BlockSpec((tm, d), row), pl.BlockSpec((1, d), fixed), _resident(wt.shape)]
    out_shape = [jax.ShapeDtypeStruct((t, d), F32)] * 3
    out_specs = [pl.BlockSpec((tm, d), row)] * 3
    args = [x, g.reshape(1, d), wt]
    if fox:
        wft, bf = fox_extra
        nh = wft.shape[0]
        in_specs += [pl.BlockSpec(wft.shape, fixed), pl.BlockSpec(bf.shape, fixed)]
        args += [wft, bf]
        out_shape = out_shape + [jax.ShapeDtypeStruct((t, nh), F32)]
        out_specs = out_specs + [pl.BlockSpec((tm, nh), row)]
    return pl.pallas_call(
        functools.partial(_inproj_sample_body, fox=fox, d=d),
        grid=(t // tm,), in_specs=in_specs, out_specs=out_specs, out_shape=out_shape,
        compiler_params=_params("parallel"), name="inproj_sample_fox" if fox else "inproj_sample_sb",
    )(*args)


CS_BLOCK = 256


def _cumsum_body(lt_ref, ct_ref, *, s):
    r = lax.broadcasted_iota(jnp.int32, (CS_BLOCK, CS_BLOCK), 0)
    c = lax.broadcasted_iota(jnp.int32, (CS_BLOCK, CS_BLOCK), 1)
    upper = jnp.where(r <= c, 1.0, 0.0).astype(F32)
    carry = jnp.zeros((lt_ref.shape[0], 1), F32)
    for j in range(s // CS_BLOCK):
        blk = lt_ref[:, j * CS_BLOCK:(j + 1) * CS_BLOCK]
        cs = jnp.dot(blk, upper, precision=lax.Precision.HIGHEST, preferred_element_type=F32) + carry
        ct_ref[:, j * CS_BLOCK:(j + 1) * CS_BLOCK] = cs
        carry = cs[:, CS_BLOCK - 1:CS_BLOCK]


def _seq_cumsum(lft_stack, layer):
    _, b, nh, s = lft_stack.shape
    return pl.pallas_call(
        functools.partial(_cumsum_body, s=s), grid=(b,),
        in_specs=[pl.BlockSpec((None, None, nh, s), lambda i: (layer, i, 0, 0))],
        out_specs=pl.BlockSpec((None, nh, s), lambda i: (i, 0, 0)),
        out_shape=jax.ShapeDtypeStruct((b, nh, s), F32), compiler_params=_params("parallel"), name="fox_cumsum",
    )(lft_stack)


def _stack_heads(q, tq):
    lane = lax.broadcasted_iota(jnp.int32, (tq, LANES), 1)
    q = q.astype(F32)
    qa = jnp.where(lane < HEAD_DIM, q, 0.0).astype(BF16)
    qb = jnp.where(lane >= HEAD_DIM, q, 0.0).astype(BF16)
    return jnp.concatenate([qa, qb], axis=0)


def _unstack_heads(acc, tq):
    lane = lax.broadcasted_iota(jnp.int32, (tq, LANES), 1)
    return jnp.where(lane < HEAD_DIM, acc[0:tq], acc[tq:2 * tq])


def _fox_attn_body(q_ref, kt_ref, vt_ref, ct_ref, o_ref, k16, v16, *, tq, s):
    k16[...] = kt_ref[...].astype(BF16)
    v16[...] = vt_ref[...].astype(BF16)
    row = lax.broadcasted_iota(jnp.int32, (tq, tq), 0)
    col = lax.broadcasted_iota(jnp.int32, (tq, tq), 1)
    causal = col <= row
    for qi in range(s // tq):
        r0, r1 = qi * tq, (qi + 1) * tq
        qs = _stack_heads(q_ref[r0:r1, :], tq)
        c = ct_ref[:, 0:r1]
        bias = c[:, r1 - 1:r1] - c
        sc = jnp.dot(qs, k16[:, 0:r1], preferred_element_type=F32)
        ps, ls = [], []
        for hh in range(2):
            sh = sc[hh * tq:(hh + 1) * tq] + bias[hh:hh + 1, :]
            sd = jnp.where(causal, sh[:, r0:r1], NEG_BIG)
            sh = jnp.concatenate([sh[:, 0:r0], sd], axis=1) if qi else sd
            p = jnp.exp(sh - jnp.max(sh, axis=1, keepdims=True))
            ls.append(jnp.sum(p, axis=1, keepdims=True))
            ps.append(p.astype(BF16))
        pv = _nt(jnp.concatenate(ps, axis=0), v16[:, 0:r1])
        o = pv / jnp.concatenate(ls, axis=0)
        o_ref[r0:r1, :] = _unstack_heads(o, tq).astype(o_ref.dtype)


def _sb_attn_body(q_ref, kt_ref, vt_ref, o_ref, k16, v16, r_ref, acc_ref, *, tq, s):
    k16[...] = kt_ref[...].astype(BF16)
    v16[...] = vt_ref[...].astype(BF16)
    reps = tq // LANES
    row = lax.broadcasted_iota(jnp.int32, (tq, tq), 0)
    col = lax.broadcasted_iota(jnp.int32, (tq, tq), 1)
    strict_lower = jnp.where(row > col, 1.0, 0.0).astype(BF16)
    row2 = lax.broadcasted_iota(jnp.int32, (2 * tq, tq), 0)
    col2 = lax.broadcasted_iota(jnp.int32, (2 * tq, tq), 1)
    valid = col2 < jnp.where(row2 >= tq, row2 - tq, row2)

    def visit(qs, c0, c1, diag):
        z = jnp.dot(qs, k16[:, c0:c1], preferred_element_type=F32)
        lk = _log_sigmoid(-z)
        nb = (c1 - c0) // tq
        r = r_ref[...]
        weights = [None] * nb
        for jb in reversed(range(nb)):
            masked = diag and jb == nb - 1
            lk_b = lk[:, jb * tq:(jb + 1) * tq]
            if masked:
                lk_b = jnp.where(valid, lk_b, 0.0)
            hi, lo = _split_bf16(lk_b)
            later = (jnp.dot(hi, strict_lower, preferred_element_type=F32)
                     + jnp.dot(lo, strict_lower, preferred_element_type=F32))
            a = jnp.exp(lk_b + z[:, jb * tq:(jb + 1) * tq] + later + jnp.tile(r, (1, reps)))
            if masked:
                a = jnp.where(valid, a, 0.0)
            weights[jb] = a.astype(BF16)
            r = r + jnp.sum(lk_b, axis=1, keepdims=True)
        w = weights[0] if nb == 1 else jnp.concatenate(weights, axis=1)
        acc_ref[...] += _nt(w, v16[:, c0:c1])
        r_ref[...] = r

    for qi in range(s // tq):
        r0, r1 = qi * tq, (qi + 1) * tq
        qs = _stack_heads(q_ref[r0:r1, :], tq)
        r_ref[...] = jnp.zeros(r_ref.shape, F32)
        acc_ref[...] = jnp.zeros(acc_ref.shape, F32)
        near = max(0, r0 - tq)
        visit(qs, near, r1, True)
        if near > 0:
            @pl.when(jnp.max(r_ref[...]) > SB_CUTOFF)
            def _():
                visit(qs, 0, near, False)
        o_ref[r0:r1, :] = _unstack_heads(acc_ref[...], tq).astype(o_ref.dtype)


def _prompt_attn(q, kt_stack, vt_stack, layer, ct, b, s):
    t, d = q.shape
    hp = d // LANES
    tq = _pick_tile(s, (256, 128))
    q_spec = pl.BlockSpec((None, s, LANES), lambda bi, hi: (bi, 0, hi))
    kv_spec = pl.BlockSpec((None, None, LANES, s), lambda bi, hi: (layer, bi, hi, 0))
    in_specs = [q_spec, kv_spec, kv_spec]
    args = [q.reshape(b, s, d), kt_stack, vt_stack]
    scratch = [pltpu.VMEM((LANES, s), BF16), pltpu.VMEM((LANES, s), BF16)]
    if ct is not None:
        in_specs.append(pl.BlockSpec((None, None, 2, s), lambda bi, hi: (bi, hi, 0, 0)))
        args.append(ct.reshape(b, hp, 2, s))
        body, name = _fox_attn_body, "fox_attn"
    else:
        scratch += [pltpu.VMEM((2 * tq, LANES), F32), pltpu.VMEM((2 * tq, LANES), F32)]
        body, name = _sb_attn_body, "sb_attn"
    o = pl.pallas_call(
        functools.partial(body, tq=tq, s=s), grid=(b, hp), in_specs=in_specs, out_specs=q_spec,
        out_shape=jax.ShapeDtypeStruct((b, s, d), BF16), scratch_shapes=scratch,
        compiler_params=_params("parallel", "parallel"), name=name,
    )(*args)
    return o.reshape(t, d)


def _mlp_body(o_ref, x_ref, wo_ref, g_ref, wu_ref, wd_ref, gf_ref, y_ref, *, final, chunk):
    x1 = x_ref[...] + jnp.dot(o_ref[...].astype(BF16), wo_ref[...], preferred_element_type=F32)
    h = _rms(x1, g_ref[...]).astype(BF16)
    acc = x1
    for c in range(wu_ref.shape[1] // chunk):
        up = jnp.dot(h, wu_ref[:, c * chunk:(c + 1) * chunk], preferred_element_type=F32)
        act = jnp.square(jnp.maximum(up, 0.0)).astype(BF16)
        acc = acc + jnp.dot(act, wd_ref[c * chunk:(c + 1) * chunk, :], preferred_element_type=F32)
    if final:
        acc = _rms(acc, gf_ref[...])
    y_ref[...] = acc


def _outproj_mlp(o, x, w_out, g, w_up, w_down, g_final, final):
    t, d = x.shape
    dff = w_up.shape[1]
    tm = _pick_tile(t, (512, 256, 128, 64, 32, 16, 8))
    row = lambda i: (i, 0)
    fixed = lambda i: (0, 0)
    return pl.pallas_call(
        functools.partial(_mlp_body, final=final, chunk=min(dff, 1024)),
        grid=(t // tm,),
        in_specs=[pl.BlockSpec((tm, d), row), pl.BlockSpec((tm, d), row), _resident((d, d)),
                  pl.BlockSpec((1, d), fixed), _resident((d, dff)), _resident((dff, d)),
                  pl.BlockSpec((1, d), fixed)],
        out_specs=pl.BlockSpec((tm, d), row), out_shape=jax.ShapeDtypeStruct((t, d), F32),
        compiler_params=_params("parallel"), name="outproj_mlp",
    )(o, x, w_out, g.reshape(1, d), w_up, w_down, g_final.reshape(1, d))


def _head_mask(d):
    head = lax.broadcasted_iota(jnp.int32, (N_HEADS, d), 0)
    lane = lax.broadcasted_iota(jnp.int32, (N_HEADS, d), 1)
    return lax.shift_right_logical(lane, 6) == head


def _page_copies(pt_ref, b, slot, layer, n_pages, pairs, sems):
    cps = []
    for j in range(n_pages):
        pg = pt_ref[b * n_pages + j]
        for n, (src, dst) in enumerate(pairs):
            cps.append(pltpu.make_async_copy(src.at[layer, pg], dst.at[slot, j], sems.at[n, slot]))
    return cps


def _prefetch_pages(copies_for, b, nb):
    slot = b % 2

    @pl.when(b == 0)
    def _():
        for c in copies_for(0, 0):
            c.start()

    @pl.when(b + 1 < nb)
    def _():
        for c in copies_for(b + 1, 1 - slot):
            c.start()

    for c in copies_for(b, slot):
        c.wait()
    return slot


def _dec_fox_body(pt_ref, q_ref, kn_ref, vn_ref, lfn_ref, ck_hbm, cv_hbm, lf_hbm, o_ref,
                  kbuf, vbuf, lfbuf, sems, *, layer, n_pages, page):
    b = pl.program_id(0)
    d = q_ref.shape[-1]
    pairs = [(ck_hbm, kbuf), (cv_hbm, vbuf), (lf_hbm, lfbuf)]
    slot = _prefetch_pages(lambda bb, sl: _page_copies(pt_ref, bb, sl, layer, n_pages, pairs, sems),
                           b, pl.num_programs(0))
    hmask = _head_mask(d)
    qm = jnp.where(hmask, jnp.broadcast_to(q_ref[...], (N_HEADS, d)), 0.0)
    r = lax.broadcasted_iota(jnp.int32, (page, page), 0)
    c = lax.broadcasted_iota(jnp.int32, (page, page), 1)
    lower_incl = jnp.where(r >= c, 1.0, 0.0).astype(BF16)
    lf_all = jnp.concatenate([lfbuf[slot, j] for j in range(n_pages)], axis=0)
    hi, lo = _split_bf16(lf_all)
    incl = (jnp.dot(hi, lower_incl, preferred_element_type=F32)
            + jnp.dot(lo, lower_incl, preferred_element_type=F32))
    excl = incl - lf_all
    suffix = jnp.zeros((N_HEADS, 1), F32)
    logits = [None] * n_pages
    for j in reversed(range(n_pages)):
        rows = slice(j * N_HEADS, (j + 1) * N_HEADS)
        s_j = jnp.dot(qm, kbuf[slot, j], preferred_element_type=F32)
        logits[j] = s_j + excl[rows] + suffix
        suffix = suffix + incl[rows][:, 0:1]
    kn = jnp.broadcast_to(kn_ref[...], (8, d))
    s_new = _nt(qm, kn)[:, 0:1] - lfn_ref[...]
    m = s_new
    for lg in logits:
        m = jnp.maximum(m, jnp.max(lg, axis=1, keepdims=True))
    p_new = jnp.exp(s_new - m)
    denom = p_new
    out = p_new * vn_ref[...]
    for j, lg in enumerate(logits):
        p = jnp.exp(lg - m)
        denom = denom + jnp.sum(p, axis=1, keepdims=True)
        out = out + _nt(p, vbuf[slot, j])
    out = jnp.where(hmask, out / denom, 0.0)
    o_ref[...] = jnp.sum(out, axis=0, keepdims=True)


SB_NEAR_PAGES = 2


def _sb_visit_pages(qm, k_pages, v_pages, r, strict_lower):
    z_all = jnp.concatenate([jnp.dot(qm, kp, preferred_element_type=F32) for kp in k_pages], axis=0)
    lk_all = _log_sigmoid(-z_all)
    hi, lo = _split_bf16(lk_all)
    later_all = (jnp.dot(hi, strict_lower, preferred_element_type=F32)
                 + jnp.dot(lo, strict_lower, preferred_element_type=F32))
    e_all = lk_all + z_all + later_all
    page_tot = jnp.sum(lk_all, axis=1, keepdims=True)
    out = jnp.zeros(qm.shape, F32)
    for j in reversed(range(len(k_pages))):
        rows = slice(j * N_HEADS, (j + 1) * N_HEADS)
        out = out + _nt(jnp.exp(e_all[rows] + r), v_pages[j])
        r = r + page_tot[rows]
    return out, r


def _dec_sb_body(pt_ref, q_ref, ck_hbm, cv_hbm, o_ref, kbuf, vbuf, kfar, vfar, acc_ref, sems,
                 *, layer, n_pages, page):
    b = pl.program_id(0)
    d = q_ref.shape[-1]
    near = min(SB_NEAR_PAGES, n_pages)
    far = n_pages - near

    def near_copies(bb, sl):
        cps = []
        for j in range(near):
            pg = pt_ref[bb * n_pages + far + j]
            cps.append(pltpu.make_async_copy(ck_hbm.at[layer, pg], kbuf.at[sl, j], sems.at[0, sl]))
            cps.append(pltpu.make_async_copy(cv_hbm.at[layer, pg], vbuf.at[sl, j], sems.at[1, sl]))
        return cps

    slot = _prefetch_pages(near_copies, b, pl.num_programs(0))
    hmask = _head_mask(d)
    qm = jnp.where(hmask, jnp.broadcast_to(q_ref[...], (N_HEADS, d)), 0.0)
    row = lax.broadcasted_iota(jnp.int32, (page, page), 0)
    col = lax.broadcasted_iota(jnp.int32, (page, page), 1)
    strict_lower = jnp.where(row > col, 1.0, 0.0).astype(BF16)
    out, r = _sb_visit_pages(qm, [kbuf[slot, j] for j in range(near)], [vbuf[slot, j] for j in range(near)],
                             jnp.zeros((N_HEADS, 1), F32), strict_lower)
    acc_ref[...] = out
    if far > 0:
        @pl.when(jnp.max(r) > SB_CUTOFF)
        def _():
            cps = []
            for j in range(far):
                pg = pt_ref[b * n_pages + j]
                cps.append(pltpu.make_async_copy(ck_hbm.at[layer, pg], kfar.at[j], sems.at[2, 0]))
                cps.append(pltpu.make_async_copy(cv_hbm.at[layer, pg], vfar.at[j], sems.at[3, 0]))
            for cp in cps:
                cp.start()
            for cp in cps:
                cp.wait()
            more, _ = _sb_visit_pages(qm, [kfar[j] for j in range(far)], [vfar[j] for j in range(far)],
                                      r, strict_lower)
            acc_ref[...] += more
    o_ref[...] = jnp.sum(jnp.where(hmask, acc_ref[...], 0.0), axis=0, keepdims=True)


def _decode_attn(page_table, q, cache_k, cache_v, layer, fox_extra):
    db, d = q.shape
    n_pages = page_table.shape[1]
    page = cache_k.shape[-1]
    vec = pl.BlockSpec((None, 1, d), lambda b, pt: (b, 0, 0))
    hbm = pl.BlockSpec(memory_space=pl.ANY)
    kv_buf = pltpu.VMEM((2, n_pages, d, page), F32)
    fox = fox_extra is not None
    if fox:
        k_new, v_new, lf_new, cache_lf = fox_extra
        nh = lf_new.shape[1]
        in_specs = [vec, vec, vec, pl.BlockSpec((None, nh, 1), lambda b, pt: (b, 0, 0)), hbm, hbm, hbm]
        args = [q.reshape(db, 1, d), k_new.reshape(db, 1, d), v_new.reshape(db, 1, d),
                lf_new.reshape(db, nh, 1), cache_k, cache_v, cache_lf]
        scratch = [kv_buf, kv_buf, pltpu.VMEM((2, n_pages, nh, page), F32), pltpu.SemaphoreType.DMA((3, 2))]
        body, name = _dec_fox_body, "decode_fox"
    else:
        in_specs = [vec, hbm, hbm]
        args = [q.reshape(db, 1, d), cache_k, cache_v]
        near = min(SB_NEAR_PAGES, n_pages)
        near_buf = pltpu.VMEM((2, near, d, page), F32)
        far_buf = pltpu.VMEM((max(n_pages - near, 1), d, page), F32)
        scratch = [near_buf, near_buf, far_buf, far_buf, pltpu.VMEM((N_HEADS, d), F32),
                   pltpu.SemaphoreType.DMA((4, 2))]
        body, name = _dec_sb_body, "decode_sb"
    o = pl.pallas_call(
        functools.partial(body, layer=layer, n_pages=n_pages, page=page),
        grid_spec=pltpu.PrefetchScalarGridSpec(
            num_scalar_prefetch=1, grid=(db,), in_specs=in_specs, out_specs=vec, scratch_shapes=scratch),
        out_shape=jax.ShapeDtypeStruct((db, 1, d), F32),
        compiler_params=_params("arbitrary"), name=name,
    )(page_table.reshape(-1), *args)
    return o.reshape(db, d)


def _feature_major(cache):
    n, pool, page, nh, dh = cache.shape
    return jnp.transpose(cache, (0, 1, 3, 4, 2)).reshape(n, pool, nh * dh, page)


def kernel(x_prompt, x_sample, cache_fox_k, cache_fox_v, cache_fox_logf, cache_sb_k, cache_sb_v, page_table,
           norm_mix, norm_mlp, norm_final, fox_w_in, fox_b_f, fox_w_out, sb_w_in, sb_w_out, mlp_w_up,
           mlp_w_down):
    b, s, d = x_prompt.shape
    db = x_sample.shape[0]
    depth = norm_mix.shape[0]
    nh = fox_b_f.shape[1]
    n_fox, n_sb = fox_w_in.shape[0], sb_w_in.shape[0]
    xp = x_prompt.reshape(b * s, d)
    xs = x_sample.reshape(db, d)
    fox_kt, fox_vt, sb_kt, sb_vt = (_feature_major(c) for c in (cache_fox_k, cache_fox_v, cache_sb_k, cache_sb_v))
    fox_lft = jnp.transpose(cache_fox_logf, (0, 1, 3, 2))
    fox_stacks = [jnp.zeros((n_fox, b, d, s), F32), jnp.zeros((n_fox, b, d, s), F32),
                  jnp.zeros((n_fox, b, nh, s), F32)]
    sb_stacks = [jnp.zeros((n_sb, b, d, s), F32), jnp.zeros((n_sb, b, d, s), F32)]
    outs = {name: [] for name in ("fks", "fvs", "fls", "sks", "svs")}
    for l in range(depth):
        i = l // 2
        final = l == depth - 1
        if l % 2 == 0:
            wt = fox_w_in[i].T.astype(BF16)
            w_qkv_t, wft = wt[:3 * d], wt[3 * d:]
            qp, *fox_stacks = _inproj_prompt(xp, norm_mix[l], w_qkv_t, (wft, fox_b_f[i].reshape(nh, 1)),
                                             fox_stacks, i, b, s)
            qs, ks, vs, lfs = _inproj_sample(xs, norm_mix[l], w_qkv_t, (wft, fox_b_f[i].reshape(1, nh)))
            op = _prompt_attn(qp, fox_stacks[0], fox_stacks[1], i, _seq_cumsum(fox_stacks[2], i), b, s)
            os_ = _decode_attn(page_table, qs, fox_kt, fox_vt, i, (ks, vs, lfs, fox_lft))
            w_out = fox_w_out[i]
            outs["fks"].append(ks); outs["fvs"].append(vs); outs["fls"].append(lfs)
        else:
            w_qkv_t = sb_w_in[i].T.astype(BF16)
            qp, *sb_stacks = _inproj_prompt(xp, norm_mix[l], w_qkv_t, None, sb_stacks, i, b, s)
            qs, ks, vs = _inproj_sample(xs, norm_mix[l], w_qkv_t, None)
            op = _prompt_attn(qp, sb_stacks[0], sb_stacks[1], i, None, b, s)
            os_ = _decode_attn(page_table, qs, sb_kt, sb_vt, i, None)
            w_out = sb_w_out[i]
            outs["sks"].append(ks); outs["svs"].append(vs)
        w_out, w_up, w_down = w_out.astype(BF16), mlp_w_up[l].astype(BF16), mlp_w_down[l].astype(BF16)
        xp = _outproj_mlp(op, xp, w_out, norm_mlp[l], w_up, w_down, norm_final, final)
        xs = _outproj_mlp(os_, xs, w_out, norm_mlp[l], w_up, w_down, norm_final, final)

    def prompt_kv(stack):
        return jnp.transpose(stack.reshape(stack.shape[0], b, N_HEADS, HEAD_DIM, s), (0, 1, 4, 2, 3))

    def sample_kv(name):
        return jnp.stack(outs[name]).reshape(-1, db, 1, N_HEADS, HEAD_DIM)

    return (xp.reshape(b, s, d), xs.reshape(db, 1, d),
            prompt_kv(fox_stacks[0]), prompt_kv(fox_stacks[1]), jnp.transpose(fox_stacks[2], (0, 1, 3, 2)),
            prompt_kv(sb_stacks[0]), prompt_kv(sb_stacks[1]),
            sample_kv("fks"), sample_kv("fvs"), jnp.stack(outs["fls"]).reshape(-1, db, 1, nh),
            sample_kv("sks"), sample_kv("svs"))
```
